```python
import jax, jax.numpy as jnp
from jax import lax
import numpy as np

D_MODEL = 1024
BATCH = 8
SEQ = 8192
DEPTH = 1

CHUNK = 64
GMLP_BLOCK = 128
GMLP_WIDTH = D_MODEL
GMLP_GROUPS = 8
GMLP_GROUP_DIM = GMLP_WIDTH // GMLP_GROUPS
LRU_WIDTH = D_MODEL
LRU_HEADS = 16
LRU_HEAD_DIM = LRU_WIDTH // LRU_HEADS
CONV_WIDTH = 4
LRU_C = 8.0
D_FF = ((-(-8 * D_MODEL // 3) + 255) // 256) * 256
PLE_DIM = 256
EPS = 1e-6
SPLITS = (GMLP_WIDTH, 2 * GMLP_WIDTH, 2 * GMLP_WIDTH + LRU_WIDTH,
          2 * GMLP_WIDTH + 2 * LRU_WIDTH, 2 * GMLP_WIDTH + 2 * LRU_WIDTH + D_MODEL)
IN_COLS = 2 * GMLP_WIDTH + 2 * LRU_WIDTH + 2 * D_MODEL

kernel_name = "hybrid_gmlp_rglru_sandwich_block"


def rms_norm(x, g):
    xf = x.astype(jnp.float32)
    y = xf * lax.rsqrt(jnp.mean(xf * xf, axis=-1, keepdims=True) + EPS)
    return (y * g.astype(jnp.float32)).astype(x.dtype)


def layer_norm(x, g, b):
    xf = x.astype(jnp.float32)
    mu = jnp.mean(xf, axis=-1, keepdims=True)
    var = jnp.mean(jnp.square(xf - mu), axis=-1, keepdims=True)
    y = (xf - mu) * lax.rsqrt(var + EPS)
    return (y * g.astype(jnp.float32) + b.astype(jnp.float32)).astype(x.dtype)


def gmlp_spatial_gate(u, v, ln_g, ln_b, w_s, b_s):
    bsz, seq, _ = v.shape
    n_blk = seq // GMLP_BLOCK
    v = layer_norm(v, ln_g, ln_b)
    vb = v.reshape(bsz, n_blk, GMLP_BLOCK, GMLP_GROUPS, GMLP_GROUP_DIM)
    chunk_id = jnp.arange(GMLP_BLOCK) // CHUNK
    mask = chunk_id[None, :] <= chunk_id[:, None]
    w = jnp.where(mask[None], w_s, jnp.zeros_like(w_s))
    mixed = jnp.einsum('gij,bnjgc->bnigc', w, vb) + b_s.T[None, None, :, :, None]
    return u * mixed.reshape(bsz, seq, GMLP_WIDTH)


def causal_depthwise_conv(x, w, b):
    c = x.shape[-1]
    y = lax.conv_general_dilated(
        x, w[:, None, :].astype(x.dtype), window_strides=(1,),
        padding=[(CONV_WIDTH - 1, 0)],
        dimension_numbers=('NWC', 'WIO', 'NWC'), feature_group_count=c)
    return y + b


def block_diag_linear(x, w, b):
    bsz, seq, _ = x.shape
    xh = x.reshape(bsz, seq, LRU_HEADS, LRU_HEAD_DIM)
    y = jnp.einsum('bshi,hij->bshj', xh, w).reshape(bsz, seq, LRU_WIDTH)
    return y + b


def rg_lru(x, w_r, b_r, w_i, b_i, lam):
    r = jax.nn.sigmoid(block_diag_linear(x, w_r, b_r).astype(jnp.float32))
    i = jax.nn.sigmoid(block_diag_linear(x, w_i, b_i).astype(jnp.float32))
    log_a = -LRU_C * r * jax.nn.softplus(-lam.astype(jnp.float32))
    a = jnp.exp(log_a)
    gated_x = jnp.sqrt(-jnp.expm1(2.0 * log_a)) * (i * x.astype(jnp.float32))

    def combine(left, right):
        a1, h1 = left
        a2, h2 = right
        return a1 * a2, a2 * h1 + h2

    _, h = lax.associative_scan(combine, (a, gated_x), axis=1)
    return h.astype(x.dtype)


def setup_inputs(seed: int = 0) -> dict:
    key = jax.random.key(seed)
    ks = iter(jax.random.split(key, 40))

    def nrm(shape, fan_in):
        return jax.random.normal(next(ks), shape, jnp.float32) * (fan_in ** -0.5)

    def gain(shape):
        return 1.0 + 0.05 * jax.random.normal(next(ks), shape, jnp.float32)

    def bias(shape, scale=0.02):
        return scale * jax.random.normal(next(ks), shape, jnp.float32)

    L = DEPTH
    x = jax.random.normal(next(ks), (BATCH, SEQ, D_MODEL), jnp.float32)
    p = jax.random.normal(next(ks), (L, BATCH, SEQ, PLE_DIM), jnp.float32)
    a0 = jax.random.uniform(next(ks), (L, LRU_WIDTH), jnp.float32, minval=0.9, maxval=0.999)
    s = a0 ** (1.0 / LRU_C)
    lru_lambda = jnp.log(s) - jnp.log1p(-s)
    return {
        "x": x,
        "p": p,
        "norm_mix_pre": gain((L, D_MODEL)),
        "norm_mix_post": gain((L, D_MODEL)),
        "w_in": nrm((L, D_MODEL, IN_COLS), D_MODEL),
        "gmlp_ln_g": gain((L, GMLP_WIDTH)),
        "gmlp_ln_b": bias((L, GMLP_WIDTH)),
        "gmlp_w_s": nrm((L, GMLP_GROUPS, GMLP_BLOCK, GMLP_BLOCK), GMLP_BLOCK),
        "gmlp_b_s": 1.0 + 0.1 * jax.random.normal(next(ks), (L, GMLP_GROUPS, GMLP_BLOCK), jnp.float32),
        "conv_w": nrm((L, CONV_WIDTH, LRU_WIDTH), CONV_WIDTH),
        "conv_b": bias((L, LRU_WIDTH)),
        "lru_w_r": nrm((L, LRU_HEADS, LRU_HEAD_DIM, LRU_HEAD_DIM), LRU_HEAD_DIM),
        "lru_b_r": bias((L, LRU_WIDTH)),
        "lru_w_i": nrm((L, LRU_HEADS, LRU_HEAD_DIM, LRU_HEAD_DIM), LRU_HEAD_DIM),
        "lru_b_i": bias((L, LRU_WIDTH)),
        "lru_lambda": lru_lambda,
        "w_branch_a": nrm((L, GMLP_WIDTH, D_MODEL), GMLP_WIDTH),
        "w_branch_b": nrm((L, LRU_WIDTH, D_MODEL), LRU_WIDTH),
        "w_out": nrm((L, D_MODEL, D_MODEL), D_MODEL),
        "norm_ffn_pre": gain((L, D_MODEL)),
        "norm_ffn_post": gain((L, D_MODEL)),
        "ffn_w_gate": nrm((L, D_MODEL, D_FF), D_MODEL),
        "ffn_w_up": nrm((L, D_MODEL, D_FF), D_MODEL),
        "ffn_w_down": nrm((L, D_FF, D_MODEL), D_FF),
        "norm_ple_pre": gain((L, D_MODEL)),
        "norm_ple_post": gain((L, D_MODEL)),
        "ple_w_in": nrm((L, PLE_DIM, D_MODEL), PLE_DIM),
        "ple_w_gate": nrm((L, D_MODEL, D_MODEL), D_MODEL),
    }


def reference(x, p, norm_mix_pre, norm_mix_post, w_in, gmlp_ln_g, gmlp_ln_b, gmlp_w_s,
              gmlp_b_s, conv_w, conv_b, lru_w_r, lru_b_r, lru_w_i, lru_b_i, lru_lambda,
              w_branch_a, w_branch_b, w_out, norm_ffn_pre, norm_ffn_post, ffn_w_gate,
              ffn_w_up, ffn_w_down, norm_ple_pre, norm_ple_post, ple_w_in, ple_w_gate):
    for l in range(DEPTH):
        h = rms_norm(x, norm_mix_pre[l])
        z = h @ w_in[l]
        u, v, xr, yr, ga, gb = jnp.split(z, SPLITS, axis=-1)
        a_out = gmlp_spatial_gate(jax.nn.gelu(u), jax.nn.gelu(v), gmlp_ln_g[l], gmlp_ln_b[l],
                                  gmlp_w_s[l], gmlp_b_s[l])
        xr = causal_depthwise_conv(xr, conv_w[l], conv_b[l])
        b_out = rg_lru(xr, lru_w_r[l], lru_b_r[l], lru_w_i[l], lru_b_i[l], lru_lambda[l]) * jax.nn.gelu(yr)
        merged = jax.nn.sigmoid(ga) * (a_out @ w_branch_a[l]) + jax.nn.sigmoid(gb) * (b_out @ w_branch_b[l])
        x = x + rms_norm(merged @ w_out[l], norm_mix_post[l])
        h = rms_norm(x, norm_ffn_pre[l])
        f = (jax.nn.silu(h @ ffn_w_gate[l]) * (h @ ffn_w_up[l])) @ ffn_w_down[l]
        x = x + rms_norm(f, norm_ffn_post[l])
        gate = jax.nn.sigmoid(rms_norm(x, norm_ple_pre[l]) @ ple_w_gate[l])
        e = p[l] @ ple_w_in[l]
        x = x + rms_norm(gate * e, norm_ple_post[l])
    return x
```

```python
import functools

import jax
import jax.numpy as jnp
import numpy as np
from jax import lax
from jax.experimental import pallas as pl
from jax.experimental.pallas import tpu as pltpu

D_MODEL = 1024
CHUNK = 64
GMLP_BLOCK = 128
GMLP_GROUPS = 8
LRU_HEADS = 16
LRU_HEAD_DIM = D_MODEL // LRU_HEADS
CONV_WIDTH = 4
LRU_C = 8.0
EPS = 1e-6

LANES = 128
SUBLANES = 8
N_LANE_CHUNKS = D_MODEL // LANES
VMEM_LIMIT_BYTES = 56 * 1024 * 1024

MIX_TILE = 256
FFN_TILE = 512

_BF = jnp.bfloat16
_F32 = jnp.float32


def _dot(a, b):
    return jnp.dot(a, b, preferred_element_type=_F32)


def _rms_norm(x, g):
    y = x * lax.rsqrt(jnp.mean(x * x, axis=-1, keepdims=True) + EPS)
    return y * g


def _layer_norm(x, g, b):
    mu = jnp.mean(x, axis=-1, keepdims=True)
    var = jnp.mean(jnp.square(x - mu), axis=-1, keepdims=True)
    return (x - mu) * lax.rsqrt(var + EPS) * g + b


def _scan_rows(a, gx, h0):
    rows = a.shape[0]
    row = lax.broadcasted_iota(jnp.int32, a.shape, 0)
    d = 1
    while d < rows:
        keep = row >= d
        a_prev = jnp.where(keep, pltpu.roll(a, d, axis=0), 1.0)
        g_prev = jnp.where(keep, pltpu.roll(gx, d, axis=0), 0.0)
        gx = gx + a * g_prev
        a = a * a_prev
        d *= 2
    return gx + a * h0


def _mixer_kernel(x_ref, g_pre_ref, g_post_ref, w_in_ref, ln_g_ref, ln_b_ref, w_s_ref,
                  b_s_ref, conv_w_ref, conv_b_ref, w_ri_ref, b_r_ref, b_i_ref, lam_ref,
                  w_a_ref, w_b_ref, w_out_ref, o_ref,
                  h_scr, vn_scr, a_scr, b_scr, xr_scr, hc_scr):
    tile = x_ref.shape[1]
    n_blk = tile // GMLP_BLOCK
    d = D_MODEL

    @pl.when(pl.program_id(1) == 0)
    def _():
        xr_scr[0:SUBLANES, :] = jnp.zeros((SUBLANES, d), _F32)
        hc_scr[...] = jnp.zeros_like(hc_scr)

    x = x_ref[0]
    h_scr[...] = _rms_norm(x, g_pre_ref[...]).astype(_BF)

    def proj(k):
        return _dot(h_scr[...], w_in_ref[:, k * d:(k + 1) * d])

    vn_scr[...] = _layer_norm(jax.nn.gelu(proj(1)), ln_g_ref[...], ln_b_ref[...]).astype(_BF)
    ug = jax.nn.gelu(proj(0))
    ci = lax.broadcasted_iota(jnp.int32, (GMLP_BLOCK, GMLP_BLOCK), 0) // CHUNK
    cj = lax.broadcasted_iota(jnp.int32, (GMLP_BLOCK, GMLP_BLOCK), 1) // CHUNK
    for g in range(GMLP_GROUPS):
        cols = slice(g * LANES, (g + 1) * LANES)
        w = jnp.where(cj <= ci, w_s_ref[g], 0.0).astype(_BF)
        rhs = jnp.concatenate(
            [vn_scr[n * GMLP_BLOCK:(n + 1) * GMLP_BLOCK, cols] for n in range(n_blk)], axis=1)
        mixed = _dot(w, rhs)
        for n in range(n_blk):
            rws = slice(n * GMLP_BLOCK, (n + 1) * GMLP_BLOCK)
            m = mixed[:, n * LANES:(n + 1) * LANES] + b_s_ref[:, cols]
            a_scr[rws, cols] = (ug[rws, cols] * m).astype(_BF)
    merged = jax.nn.sigmoid(proj(4)) * _dot(a_scr[...], w_a_ref[...])

    xr_scr[SUBLANES:SUBLANES + tile, :] = proj(2)
    yg = jax.nn.gelu(proj(3))
    neg_c_sp = -LRU_C * jax.nn.softplus(-lam_ref[...])
    for c in range(N_LANE_CHUNKS):
        cols = slice(c * LANES, (c + 1) * LANES)
        xc = conv_b_ref[:, cols]
        for k in range(CONV_WIDTH):
            off = SUBLANES - (CONV_WIDTH - 1) + k
            xc = xc + conv_w_ref[k:k + 1, cols] * xr_scr[off:off + tile, cols]
        ri = _dot(xc.astype(_BF), w_ri_ref[c])
        r = jax.nn.sigmoid(ri[:, :LANES] + b_r_ref[:, cols])
        i = jax.nn.sigmoid(ri[:, LANES:] + b_i_ref[:, cols])
        log_a = neg_c_sp[:, cols] * r
        a = jnp.exp(log_a)
        gx = jnp.sqrt(-jnp.tanh(log_a) * (a * a + 1.0)) * (i * xc)
        hseq = _scan_rows(a, gx, hc_scr[:, cols])
        hc_scr[:, cols] = hseq[tile - 1:tile, :]
        b_scr[:, cols] = (hseq * yg[:, cols]).astype(_BF)
    xr_scr[0:SUBLANES, :] = xr_scr[tile:tile + SUBLANES, :]
    merged = merged + jax.nn.sigmoid(proj(5)) * _dot(b_scr[...], w_b_ref[...])

    y = _dot(merged.astype(_BF), w_out_ref[...])
    o_ref[0] = x + _rms_norm(y, g_post_ref[...])


def _ffn_ple_kernel(x_ref, p_ref, g_fpre_ref, g_fpost_ref, w_gate_ref, w_up_ref, w_down_ref,
                    g_ppre_ref, g_ppost_ref, w_pin_ref, w_pgate_ref, o_ref):
    x = x_ref[...]
    h = _rms_norm(x, g_fpre_ref[...]).astype(_BF)
    act = jax.nn.silu(_dot(h, w_gate_ref[...])) * _dot(h, w_up_ref[...])
    f = _dot(act.astype(_BF), w_down_ref[...])
    x = x + _rms_norm(f, g_fpost_ref[...])
    gate = jax.nn.sigmoid(_dot(_rms_norm(x, g_ppre_ref[...]).astype(_BF), w_pgate_ref[...]))
    e = _dot(p_ref[...].astype(_BF), w_pin_ref[...])
    o_ref[...] = x + _rms_norm(gate * e, g_ppost_ref[...])


def _resident(shape):
    zeros = (0,) * len(shape)
    return pl.BlockSpec(shape, lambda *_: zeros, pipeline_mode=pl.Buffered(1))


def _pair_block_diag(w):
    w = w.reshape(N_LANE_CHUNKS, 2, LRU_HEAD_DIM, LRU_HEAD_DIM)
    z = jnp.zeros_like(w[:, 0])
    top = jnp.concatenate([w[:, 0], z], axis=2)
    bot = jnp.concatenate([z, w[:, 1]], axis=2)
    return jnp.concatenate([top, bot], axis=1)


def _mixer(x, g_pre, g_post, w_in, ln_g, ln_b, w_s, b_s, conv_w, conv_b, w_ri, b_r, b_i, lam,
           w_a, w_b, w_out):
    bsz, seq, d = x.shape
    tile = MIX_TILE
    row = lambda v: v.reshape(1, d)
    operands = (x, row(g_pre), row(g_post), w_in, row(ln_g), row(ln_b), w_s, b_s, conv_w,
                row(conv_b), w_ri, row(b_r), row(b_i), row(lam), w_a, w_b, w_out)
    x_spec = pl.BlockSpec((1, tile, d), lambda b, t: (b, t, 0))
    in_specs = [x_spec] + [_resident(op.shape) for op in operands[1:]]
    return pl.pallas_call(
        _mixer_kernel,
        grid=(bsz, seq // tile),
        in_specs=in_specs,
        out_specs=x_spec,
        out_shape=jax.ShapeDtypeStruct(x.shape, x.dtype),
        scratch_shapes=[
            pltpu.VMEM((tile, d), _BF),
            pltpu.VMEM((tile, d), _BF),
            pltpu.VMEM((tile, d), _BF),
            pltpu.VMEM((tile, d), _BF),
            pltpu.VMEM((tile + SUBLANES, d), _F32),
            pltpu.VMEM((1, d), _F32),
        ],
        compiler_params=pltpu.CompilerParams(
            dimension_semantics=("arbitrary", "arbitrary"),
            vmem_limit_bytes=VMEM_LIMIT_BYTES),
        name="mixer",
    )(*operands)


def _ffn_ple(x, p, g_fpre, g_fpost, w_gate, w_up, w_down, g_ppre, g_ppost, w_pin, w_pgate):
    n, d = x.shape
    tile = FFN_TILE
    row = lambda v: v.reshape(1, d)
    operands = (x, p, row(g_fpre), row(g_fpost), w_gate, w_up, w_down, row(g_ppre),
                row(g_ppost), w_pin, w_pgate)
    x_spec = pl.BlockSpec((tile, d), lambda i: (i, 0))
    p_spec = pl.BlockSpec((tile, p.shape[1]), lambda i: (i, 0))
    in_specs = [x_spec, p_spec] + [_resident(op.shape) for op in operands[2:]]
    return pl.pallas_call(
        _ffn_ple_kernel,
        grid=(n // tile,),
        in_specs=in_specs,
        out_specs=x_spec,
        out_shape=jax.ShapeDtypeStruct(x.shape, x.dtype),
        compiler_params=pltpu.CompilerParams(
            dimension_semantics=("arbitrary",),
            vmem_limit_bytes=VMEM_LIMIT_BYTES),
        name="ffn_ple",
    )(*operands)


def kernel(x, p, norm_mix_pre, norm_mix_post, w_in, gmlp_ln_g, gmlp_ln_b, gmlp_w_s, gmlp_b_s, conv_w, conv_b, lru_w_r, lru_b_r, lru_w_i, lru_b_i, lru_lambda, w_branch_a, w_branch_b, w_out, norm_ffn_pre, norm_ffn_post, ffn_w_gate, ffn_w_up, ffn_w_down, norm_ple_pre, norm_ple_post, ple_w_in, ple_w_gate):
    bsz, seq, d = x.shape
    depth = w_in.shape[0]
    for l in range(depth):
        w_ri = jnp.concatenate(
            [_pair_block_diag(lru_w_r[l]), _pair_block_diag(lru_w_i[l])], axis=2).astype(_BF)
        b_s = jnp.repeat(gmlp_b_s[l].T, LANES, axis=1)
        x = _mixer(x, norm_mix_pre[l], norm_mix_post[l], w_in[l].astype(_BF), gmlp_ln_g[l],
                   gmlp_ln_b[l], gmlp_w_s[l], b_s, conv_w[l], conv_b[l], w_ri, lru_b_r[l],
                   lru_b_i[l], lru_lambda[l], w_branch_a[l].astype(_BF),
                   w_branch_b[l].astype(_BF), w_out[l].astype(_BF))
        x = _ffn_ple(x.reshape(bsz * seq, d), p[l].reshape(bsz * seq, -1), norm_ffn_pre[l],
                     norm_ffn_post[l], ffn_w_gate[l].astype(_BF), ffn_w_up[l].astype(_BF),
                     ffn_w_down[l].astype(_BF), norm_ple_pre[l], norm_ple_post[l],
                     ple_w_in[l].astype(_BF), ple_w_gate[l].astype(_BF)).reshape(bsz, seq, d)
    return x
```

```python
import jax
import jax.numpy as jnp
from jax import lax
from jax.experimental import pallas as pl
from jax.experimental.pallas import tpu as pltpu

D_MODEL = 1024
CHUNK = 64
GMLP_BLOCK = 128
GMLP_GROUPS = 8
LRU_HEADS = 16
LRU_HEAD_DIM = D_MODEL // LRU_HEADS
CONV_WIDTH = 4
LRU_C = 8.0
EPS = 1e-6

LANES = 128
SUBLANES = 8
N_LANE_CHUNKS = D_MODEL // LANES
VMEM_LIMIT_BYTES = 56 * 1024 * 1024

LRU_STEPS = 32
MIX_TILE = 256
FFN_TILE = 512

_BF = jnp.bfloat16
_F32 = jnp.float32


def _dot(a, b):
    return jnp.dot(a, b, preferred_element_type=_F32)


def _rms_norm(x, g):
    y = x * lax.rsqrt(jnp.mean(x * x, axis=-1, keepdims=True) + EPS)
    return y * g


def _layer_norm(x, g, b):
    mu = jnp.mean(x, axis=-1, keepdims=True)
    var = jnp.mean(jnp.square(x - mu), axis=-1, keepdims=True)
    return (x - mu) * lax.rsqrt(var + EPS) * g + b


def _window_copies(hbm, vmem, sem, step, slot, to_vmem):
    steps = vmem.shape[1]
    copies = []
    for b in range(hbm.shape[0]):
        in_hbm = hbm.at[b, pl.ds(step * steps, steps), :]
        in_vmem = vmem.at[slot, :, b, :]
        src, dst = (in_hbm, in_vmem) if to_vmem else (in_vmem, in_hbm)
        copies.append(pltpu.make_async_copy(src, dst, sem.at[slot]))
    return copies


def _lru_branch_kernel(x_hbm, g_pre_ref, w_xy_ref, w_gb_ref, conv_w_ref, conv_b_ref, w_ri_ref,
                       b_r_ref, b_i_ref, lam_ref, w_b_ref, o_hbm,
                       xin, out, sem_in, sem_out, xr_scr, b_scr, hc_scr):
    steps, bsz, d = xin.shape[1:]
    rows = steps * bsz
    tail = (CONV_WIDTH - 1) * bsz
    i = pl.program_id(0)
    n = pl.num_programs(0)
    slot = lax.rem(i, 2)

    def loads(step, slot):
        return _window_copies(x_hbm, xin, sem_in, step, slot, to_vmem=True)

    def stores(step, slot):
        return _window_copies(o_hbm, out, sem_out, step, slot, to_vmem=False)

    @pl.when(i == 0)
    def _():
        for c in loads(0, 0):
            c.start()
        xr_scr[0:tail, :] = jnp.zeros((tail, d), _F32)
        hc_scr[...] = jnp.zeros_like(hc_scr)

    @pl.when(i + 1 < n)
    def _():
        for c in loads(i + 1, 1 - slot):
            c.start()

    for c in loads(i, slot):
        c.wait()

    x = xin[slot].reshape(rows, d)
    hb = _rms_norm(x, g_pre_ref[...]).astype(_BF)
    xr_scr[tail:tail + rows, :] = _dot(hb, w_xy_ref[:, 0:d])
    yg = jax.nn.gelu(_dot(hb, w_xy_ref[:, d:2 * d]))
    neg_c_sp = -LRU_C * jax.nn.softplus(-lam_ref[...])
    for c in range(N_LANE_CHUNKS):
        cols = slice(c * LANES, (c + 1) * LANES)
        xc = conv_b_ref[:, cols]
        for k in range(CONV_WIDTH):
            xc = xc + conv_w_ref[k:k + 1, cols] * xr_scr[k * bsz:k * bsz + rows, cols]
        ri = _dot(xc.astype(_BF), w_ri_ref[c])
        r = jax.nn.sigmoid(ri[:, :LANES] + b_r_ref[:, cols])
        gi = jax.nn.sigmoid(ri[:, LANES:] + b_i_ref[:, cols])
        log_a = neg_c_sp[:, cols] * r
        a = jnp.exp(log_a)
        gx = jnp.sqrt(-jnp.tanh(log_a) * (a * a + 1.0)) * (gi * xc)
        h = hc_scr[:, cols]
        hs = []
        for t in range(steps):
            h = a[t * bsz:(t + 1) * bsz] * h + gx[t * bsz:(t + 1) * bsz]
            hs.append(h)
        hc_scr[:, cols] = h
        b_scr[:, cols] = (jnp.concatenate(hs, axis=0) * yg[:, cols]).astype(_BF)
    xr_scr[0:tail, :] = xr_scr[rows:rows + tail, :]
    res = jax.nn.sigmoid(_dot(hb, w_gb_ref[...])) * _dot(b_scr[...], w_b_ref[...])

    @pl.when(i >= 2)
    def _():
        for c in stores(i - 2, slot):
            c.wait()

    out[slot] = res.reshape(steps, bsz, d)
    for c in stores(i, slot):
        c.start()

    @pl.when(i == n - 1)
    def _():
        for c in stores(i - 1, 1 - slot):
            c.wait()
        for c in stores(i, slot):
            c.wait()


def _gmlp_merge_kernel(x_ref, mb_ref, g_pre_ref, g_post_ref, w_uv_ref, w_ga_ref, ln_g_ref, ln_b_ref,
                       w_s_ref, b_s_ref, w_a_ref, w_out_ref, o_ref, vn_scr, a_scr):
    tile = x_ref.shape[1]
    n_blk = tile // GMLP_BLOCK
    d = D_MODEL
    x = x_ref[0]
    hb = _rms_norm(x, g_pre_ref[...]).astype(_BF)
    vn_scr[...] = _layer_norm(jax.nn.gelu(_dot(hb, w_uv_ref[:, d:2 * d])), ln_g_ref[...],
                              ln_b_ref[...]).astype(_BF)
    ug = jax.nn.gelu(_dot(hb, w_uv_ref[:, 0:d]))
    ci = lax.broadcasted_iota(jnp.int32, (GMLP_BLOCK, GMLP_BLOCK), 0) // CHUNK
    cj = lax.broadcasted_iota(jnp.int32, (GMLP_BLOCK, GMLP_BLOCK), 1) // CHUNK
    for g in range(GMLP_GROUPS):
        cols = slice(g * LANES, (g + 1) * LANES)
        w = jnp.where(cj <= ci, w_s_ref[g], 0.0).astype(_BF)
        rhs = jnp.concatenate(
            [vn_scr[n * GMLP_BLOCK:(n + 1) * GMLP_BLOCK, cols] for n in range(n_blk)], axis=1)
        mixed = _dot(w, rhs)
        for n in range(n_blk):
            rws = slice(n * GMLP_BLOCK, (n + 1) * GMLP_BLOCK)
            m = mixed[:, n * LANES:(n + 1) * LANES] + b_s_ref[:, cols]
            a_scr[rws, cols] = (ug[rws, cols] * m).astype(_BF)
    merged = jax.nn.sigmoid(_dot(hb, w_ga_ref[...])) * _dot(a_scr[...], w_a_ref[...]) + mb_ref[0]
    y = _dot(merged.astype(_BF), w_out_ref[...])
    o_ref[0] = x + _rms_norm(y, g_post_ref[...])


def _ffn_ple_kernel(x_ref, p_ref, g_fpre_ref, g_fpost_ref, w_gate_ref, w_up_ref, w_down_ref,
                    g_ppre_ref, g_ppost_ref, w_pin_ref, w_pgate_ref, o_ref):
    x = x_ref[...]
    h = _rms_norm(x, g_fpre_ref[...]).astype(_BF)
    act = jax.nn.silu(_dot(h, w_gate_ref[...])) * _dot(h, w_up_ref[...])
    f = _dot(act.astype(_BF), w_down_ref[...])
    x = x + _rms_norm(f, g_fpost_ref[...])
    gate = jax.nn.sigmoid(_dot(_rms_norm(x, g_ppre_ref[...]).astype(_BF), w_pgate_ref[...]))
    e = _dot(p_ref[...].astype(_BF), w_pin_ref[...])
    o_ref[...] = x + _rms_norm(gate * e, g_ppost_ref[...])


def _resident(shape):
    zeros = (0,) * len(shape)
    return pl.BlockSpec(shape, lambda *_: zeros, pipeline_mode=pl.Buffered(1))


def _pair_block_diag(w):
    w = w.reshape(N_LANE_CHUNKS, 2, LRU_HEAD_DIM, LRU_HEAD_DIM)
    z = jnp.zeros_like(w[:, 0])
    top = jnp.concatenate([w[:, 0], z], axis=2)
    bot = jnp.concatenate([z, w[:, 1]], axis=2)
    return jnp.concatenate([top, bot], axis=1)


def _lru_branch(x, g_pre, w_xy, w_gb, conv_w, conv_b, w_ri, b_r, b_i, lam, w_b):
    bsz, seq, d = x.shape
    assert bsz == SUBLANES and seq % LRU_STEPS == 0
    steps = LRU_STEPS
    rows = steps * bsz
    row = lambda v: v.reshape(1, d)
    operands = (x, row(g_pre), w_xy, w_gb, conv_w, row(conv_b), w_ri, row(b_r), row(b_i), row(lam),
                w_b)
    hbm = pl.BlockSpec(memory_space=pl.ANY)
    return pl.pallas_call(
        _lru_branch_kernel,
        grid=(seq // steps,),
        in_specs=[hbm] + [_resident(op.shape) for op in operands[1:]],
        out_specs=hbm,
        out_shape=jax.ShapeDtypeStruct(x.shape, _F32),
        scratch_shapes=[
            pltpu.VMEM((2, steps, bsz, d), _F32),
            pltpu.VMEM((2, steps, bsz, d), _F32),
            pltpu.SemaphoreType.DMA((2,)),
            pltpu.SemaphoreType.DMA((2,)),
            pltpu.VMEM((rows + (CONV_WIDTH - 1) * bsz, d), _F32),
            pltpu.VMEM((rows, d), _BF),
            pltpu.VMEM((bsz, d), _F32),
        ],
        compiler_params=pltpu.CompilerParams(
            dimension_semantics=("arbitrary",),
            vmem_limit_bytes=VMEM_LIMIT_BYTES),
        name="lru_branch",
    )(*operands)


def _gmlp_merge(x, mb, g_pre, g_post, w_uv, w_ga, ln_g, ln_b, w_s, b_s, w_a, w_out):
    bsz, seq, d = x.shape
    tile = MIX_TILE
    row = lambda v: v.reshape(1, d)
    operands = (x, mb, row(g_pre), row(g_post), w_uv, w_ga, row(ln_g), row(ln_b), w_s, b_s, w_a,
                w_out)
    x_spec = pl.BlockSpec((1, tile, d), lambda b, t: (b, t, 0))
    return pl.pallas_call(
        _gmlp_merge_kernel,
        grid=(bsz, seq // tile),
        in_specs=[x_spec, x_spec] + [_resident(op.shape) for op in operands[2:]],
        out_specs=x_spec,
        out_shape=jax.ShapeDtypeStruct(x.shape, x.dtype),
        scratch_shapes=[
            pltpu.VMEM((tile, d), _BF),
            pltpu.VMEM((tile, d), _BF),
        ],
        compiler_params=pltpu.CompilerParams(
            dimension_semantics=("arbitrary", "arbitrary"),
            vmem_limit_bytes=VMEM_LIMIT_BYTES),
        name="gmlp_merge",
    )(*operands)


def _ffn_ple(x, p, g_fpre, g_fpost, w_gate, w_up, w_down, g_ppre, g_ppost, w_pin, w_pgate):
    n, d = x.shape
    tile = FFN_TILE
    row = lambda v: v.reshape(1, d)
    operands = (x, p, row(g_fpre), row(g_fpost), w_gate, w_up, w_down, row(g_ppre),
                row(g_ppost), w_pin, w_pgate)
    x_spec = pl.BlockSpec((tile, d), lambda i: (i, 0))
    p_spec = pl.BlockSpec((tile, p.shape[1]), lambda i: (i, 0))
    in_specs = [x_spec, p_spec] + [_resident(op.shape) for op in operands[2:]]
    return pl.pallas_call(
        _ffn_ple_kernel,
        grid=(n // tile,),
        in_specs=in_specs,
        out_specs=x_spec,
        out_shape=jax.ShapeDtypeStruct(x.shape, x.dtype),
        compiler_params=pltpu.CompilerParams(
            dimension_semantics=("arbitrary",),
            vmem_limit_bytes=VMEM_LIMIT_BYTES),
        name="ffn_ple",
    )(*operands)


def kernel(x, p, norm_mix_pre, norm_mix_post, w_in, gmlp_ln_g, gmlp_ln_b, gmlp_w_s, gmlp_b_s, conv_w, conv_b, lru_w_r, lru_b_r, lru_w_i, lru_b_i, lru_lambda, w_branch_a, w_branch_b, w_out, norm_ffn_pre, norm_ffn_post, ffn_w_gate, ffn_w_up, ffn_w_down, norm_ple_pre, norm_ple_post, ple_w_in, ple_w_gate):
    bsz, seq, d = x.shape
    depth = w_in.shape[0]
    for l in range(depth):
        w_ri = jnp.concatenate(
            [_pair_block_diag(lru_w_r[l]), _pair_block_diag(lru_w_i[l])], axis=2).astype(_BF)
        b_s = jnp.repeat(gmlp_b_s[l].T, LANES, axis=1)
        w_in_bf = w_in[l].astype(_BF)
        mb = _lru_branch(x, norm_mix_pre[l], w_in_bf[:, 2 * d:4 * d], w_in_bf[:, 5 * d:6 * d],
                         conv_w[l], conv_b[l], w_ri, lru_b_r[l], lru_b_i[l], lru_lambda[l],
                         w_branch_b[l].astype(_BF))
        x = _gmlp_merge(x, mb, norm_mix_pre[l], norm_mix_post[l], w_in_bf[:, 0:2 * d],
                        w_in_bf[:, 4 * d:5 * d], gmlp_ln_g[l], gmlp_ln_b[l], gmlp_w_s[l], b_s,
                        w_branch_a[l].astype(_BF), w_out[l].astype(_BF))
        x = _ffn_ple(x.reshape(bsz * seq, d), p[l].reshape(bsz * seq, -1), norm_ffn_pre[l],
                     norm_ffn_post[l], ffn_w_gate[l].astype(_BF), ffn_w_up[l].astype(_BF),
                     ffn_w_down[l].astype(_BF), norm_ple_pre[l], norm_ple_post[l],
                     ple_w_in[l].astype(_BF), ple_w_gate[l].astype(_BF)).reshape(bsz, seq, d)
    return x
```

```python
import jax
import jax.numpy as jnp
import numpy as np
from jax import lax
from jax.experimental import pallas as pl
from jax.experimental.pallas import tpu as pltpu

D_MODEL = 1024
CHUNK = 64
GMLP_BLOCK = 128
GMLP_GROUPS = 8
LRU_HEADS = 16
LRU_HEAD_DIM = D_MODEL // LRU_HEADS
CONV_WIDTH = 4
LRU_C = 8.0
EPS = 1e-6

LANES = 128
SUBLANES = 8
N_LANE_CHUNKS = D_MODEL // LANES
VMEM_LIMIT_BYTES = 56 * 1024 * 1024

SUB_ROWS = 256
MXU_COLS = 256
LRU_STEPS = 128
MIX_TILE = 1024
FFN_TILE = 1024

_BF = jnp.bfloat16
_F32 = jnp.float32


def _dot(a, b):
    return jnp.dot(a, b, preferred_element_type=_F32)


def _rms_norm(x, g):
    y = x * lax.rsqrt(jnp.mean(x * x, axis=-1, keepdims=True) + EPS)
    return y * g


def _layer_norm(x, g, b):
    mu = jnp.mean(x, axis=-1, keepdims=True)
    var = jnp.mean(jnp.square(x - mu), axis=-1, keepdims=True)
    return (x - mu) * lax.rsqrt(var + EPS) * g + b


_GELU_C1 = float(np.sqrt(2.0 / np.pi))
_GELU_C3 = float(np.sqrt(2.0 / np.pi) * 0.044715)


def _gelu(x):
    t = jnp.tanh(x * (_GELU_C1 + _GELU_C3 * (x * x)))
    hx = 0.5 * x
    return hx + hx * t


def _sqrt_nonneg(z):
    return jnp.where(z > 0.0, z * lax.rsqrt(z), 0.0)


def _window_copies(hbm, vmem, sem, step, slot, to_vmem):
    steps = vmem.shape[1]
    copies = []
    for b in range(hbm.shape[0]):
        in_hbm = hbm.at[b, pl.ds(step * steps, steps), :]
        in_vmem = vmem.at[slot, :, b, :]
        src, dst = (in_hbm, in_vmem) if to_vmem else (in_vmem, in_hbm)
        copies.append(pltpu.make_async_copy(src, dst, sem.at[slot]))
    return copies


def _emit_skewed(chains, skew):
    depth = max(len(chain) for chain in chains)
    for t in range(depth + (len(chains) - 1) * skew):
        for s, chain in enumerate(chains):
            k = t - s * skew
            if 0 <= k < len(chain):
                chain[k]()


def _lru_branch_kernel(x_hbm, g_pre_ref, w_xy_ref, w_gb_ref, conv_w_ref, conv_b_ref, w_ri_ref,
                       b_r_ref, b_i_ref, lam_ref, w_b_ref, o_hbm,
                       xin, out, sem_in, sem_out, xr_scr, b_scr, hc_scr):
    steps, bsz, d = xin.shape[1:]
    sub_steps = SUB_ROWS // bsz
    tail = (CONV_WIDTH - 1) * bsz
    n_piece = d // MXU_COLS
    chunks_per_piece = MXU_COLS // LANES
    i = pl.program_id(0)
    n = pl.num_programs(0)
    slot = lax.rem(i, 2)

    def loads(step, slot):
        return _window_copies(x_hbm, xin, sem_in, step, slot, to_vmem=True)

    def stores(step, slot):
        return _window_copies(o_hbm, out, sem_out, step, slot, to_vmem=False)

    @pl.when(i == 0)
    def _():
        for c in loads(0, 0):
            c.start()
        xr_scr[0:tail, :] = jnp.zeros((tail, d), _F32)
        hc_scr[...] = jnp.zeros_like(hc_scr)

    @pl.when(i + 1 < n)
    def _():
        for c in loads(i + 1, 1 - slot):
            c.start()

    for c in loads(i, slot):
        c.wait()

    @pl.when(i >= 2)
    def _():
        for c in stores(i - 2, slot):
            c.wait()

    neg_c_sp = -LRU_C * jax.nn.softplus(-lam_ref[...])
    h_state = [hc_scr[:, c * LANES:(c + 1) * LANES] for c in range(N_LANE_CHUNKS)]

    def sub_tile_chain(s):
        r0 = s * SUB_ROWS
        window = slice(s * sub_steps, (s + 1) * sub_steps)
        v = {}

        def norm():
            x = xin[slot, window].reshape(SUB_ROWS, d)
            v["hb"] = _rms_norm(x, g_pre_ref[...]).astype(_BF)

        def project(j):
            pc = slice(j * MXU_COLS, (j + 1) * MXU_COLS)
            xr_scr[tail + r0:tail + r0 + SUB_ROWS, pc] = _dot(v["hb"], w_xy_ref[:, pc])
            v["yr", j] = _dot(v["hb"], w_xy_ref[:, d + j * MXU_COLS:d + (j + 1) * MXU_COLS])

        def recur(j):
            yg = _gelu(v.pop(("yr", j)))
            for cc in range(chunks_per_piece):
                c = j * chunks_per_piece + cc
                cols = slice(c * LANES, (c + 1) * LANES)
                xc = conv_b_ref[:, cols]
                for k in range(CONV_WIDTH):
                    xc = xc + conv_w_ref[k:k + 1, cols] * xr_scr[r0 + k * bsz:r0 + k * bsz + SUB_ROWS, cols]
                ri = _dot(xc.astype(_BF), w_ri_ref[c])
                r = jax.nn.sigmoid(ri[:, :LANES] + b_r_ref[:, cols])
                gi = jax.nn.sigmoid(ri[:, LANES:] + b_i_ref[:, cols])
                log_a = neg_c_sp[:, cols] * r
                a = jnp.exp(log_a)
                gx = _sqrt_nonneg(-jnp.tanh(log_a) * (a * a + 1.0)) * (gi * xc)
                h = h_state[c]
                hs = []
                for t in range(sub_steps):
                    h = a[t * bsz:(t + 1) * bsz] * h + gx[t * bsz:(t + 1) * bsz]
                    hs.append(h)
                h_state[c] = h
                gated = jnp.concatenate(hs, axis=0) * yg[:, cc * LANES:(cc + 1) * LANES]
                b_scr[r0:r0 + SUB_ROWS, cols] = gated.astype(_BF)

        def gate():
            v["gb"] = jax.nn.sigmoid(_dot(v["hb"], w_gb_ref[:, 0:d]))

        def out_project():
            res = v.pop("gb") * _dot(b_scr[r0:r0 + SUB_ROWS, 0:d], w_b_ref[:, 0:d])
            out[slot, window] = res.reshape(sub_steps, bsz, d)

        def first():
            norm()
            project(0)

        def middle(j):
            def stage():
                project(j + 1)
                if j + 2 == n_piece:
                    gate()
                recur(j)
            return stage

        return ([first] + [middle(j) for j in range(n_piece - 1)]
                + [lambda: recur(n_piece - 1), out_project])

    chains = [sub_tile_chain(s) for s in range(steps // sub_steps)]
    _emit_skewed(chains, skew=len(chains[0]) - 2)
    for c in range(N_LANE_CHUNKS):
        hc_scr[:, c * LANES:(c + 1) * LANES] = h_state[c]
    xr_scr[0:tail, :] = xr_scr[steps * bsz:steps * bsz + tail, :]

    for c in stores(i, slot):
        c.start()

    @pl.when(i == n - 1)
    def _():
        for c in stores(i - 1, 1 - slot):
            c.wait()
        for c in stores(i, slot):
            c.wait()


def _gmlp_merge_kernel(x_ref, mb_ref, g_pre_ref, g_post_ref, w_uv_ref, w_ga_ref, ln_g_ref, ln_b_ref,
                       w_s_ref, b_s_ref, w_a_ref, w_out_ref, o_ref, vn_scr, a_scr):
    tile = x_ref.shape[1]
    n_blk = SUB_ROWS // GMLP_BLOCK
    d = D_MODEL
    ci = lax.broadcasted_iota(jnp.int32, (GMLP_BLOCK, GMLP_BLOCK), 0) // CHUNK
    cj = lax.broadcasted_iota(jnp.int32, (GMLP_BLOCK, GMLP_BLOCK), 1) // CHUNK
    w_s = [jnp.where(cj <= ci, w_s_ref[g], 0.0).astype(_BF) for g in range(GMLP_GROUPS)]

    def sub_tile_chain(s):
        rows = slice(s * SUB_ROWS, (s + 1) * SUB_ROWS)
        r0 = s * SUB_ROWS
        v = {}

        def norm_v():
            v["hb"] = _rms_norm(x_ref[0, rows, :], g_pre_ref[...]).astype(_BF)
            v["v"] = _dot(v["hb"], w_uv_ref[:, d:2 * d])

        def ln_u():
            v["u"] = _dot(v["hb"], w_uv_ref[:, 0:d])
            vn_scr[rows, 0:d] = _layer_norm(_gelu(v.pop("v")), ln_g_ref[...], ln_b_ref[...]).astype(_BF)

        def spatial():
            v["ga"] = _dot(v["hb"], w_ga_ref[:, 0:d])
            ug = _gelu(v.pop("u"))
            for g in range(GMLP_GROUPS):
                cols = slice(g * LANES, (g + 1) * LANES)
                rhs = jnp.concatenate(
                    [vn_scr[r0 + n * GMLP_BLOCK:r0 + (n + 1) * GMLP_BLOCK, cols] for n in range(n_blk)],
                    axis=1)
                mixed = _dot(w_s[g], rhs)
                for n in range(n_blk):
                    blk = slice(n * GMLP_BLOCK, (n + 1) * GMLP_BLOCK)
                    m = mixed[:, n * LANES:(n + 1) * LANES] + b_s_ref[:, cols]
                    a_scr[r0 + n * GMLP_BLOCK:r0 + (n + 1) * GMLP_BLOCK, cols] = (ug[blk, cols] * m).astype(_BF)

        def a_project():
            v["m"] = (jax.nn.sigmoid(v.pop("ga")) * _dot(a_scr[rows, 0:d], w_a_ref[:, 0:d])
                      + mb_ref[0, rows, :])

        def o_project():
            v["y"] = _dot(v.pop("m").astype(_BF), w_out_ref[:, 0:d])

        def residual():
            o_ref[0, rows, :] = x_ref[0, rows, :] + _rms_norm(v.pop("y"), g_post_ref[...])

        return [norm_v, ln_u, spatial, a_project, o_project, residual]

    chains = [sub_tile_chain(s) for s in range(tile // SUB_ROWS)]
    _emit_skewed(chains, skew=len(chains[0]) // 2)


def _ffn_ple_kernel(x_ref, p_ref, g_fpre_ref, g_fpost_ref, w_gate_ref, w_up_ref, w_down_ref,
                    g_ppre_ref, g_ppost_ref, w_pin_ref, w_pgate_ref, o_ref):
    d = D_MODEL
    tile = x_ref.shape[0]

    def sub_tile_chain(s):
        rows = slice(s * SUB_ROWS, (s + 1) * SUB_ROWS)
        v = {}

        def up():
            h = _rms_norm(x_ref[rows, :], g_fpre_ref[...]).astype(_BF)
            v["g"] = _dot(h, w_gate_ref[...])
            v["u"] = _dot(h, w_up_ref[...])
            v["e"] = _dot(p_ref[rows, :].astype(_BF), w_pin_ref[:, 0:d])

        def down():
            act = jax.nn.silu(v.pop("g")) * v.pop("u")
            v["f"] = _dot(act.astype(_BF), w_down_ref[:, 0:d])

        def ple_gate():
            x = x_ref[rows, :] + _rms_norm(v.pop("f"), g_fpost_ref[...])
            v["x"] = x
            v["pg"] = _dot(_rms_norm(x, g_ppre_ref[...]).astype(_BF), w_pgate_ref[:, 0:d])

        def residual():
            gate = jax.nn.sigmoid(v.pop("pg"))
            o_ref[rows, :] = v.pop("x") + _rms_norm(gate * v.pop("e"), g_ppost_ref[...])

        return [up, down, ple_gate, residual]

    chains = [sub_tile_chain(s) for s in range(tile // SUB_ROWS)]
    _emit_skewed(chains, skew=1)


def _resident(shape):
    zeros = (0,) * len(shape)
    return pl.BlockSpec(shape, lambda *_: zeros, pipeline_mode=pl.Buffered(1))


def _odd_pitch(w):
    return jnp.pad(w, ((0, 0), (0, LANES)))


def _pair_block_diag(w):
    w = w.reshape(N_LANE_CHUNKS, 2, LRU_HEAD_DIM, LRU_HEAD_DIM)
    z = jnp.zeros_like(w[:, 0])
    top = jnp.concatenate([w[:, 0], z], axis=2)
    bot = jnp.concatenate([z, w[:, 1]], axis=2)
    return jnp.concatenate([top, bot], axis=1)


def _lru_branch(x, g_pre, w_xy, w_gb, conv_w, conv_b, w_ri, b_r, b_i, lam, w_b):
    bsz, seq, d = x.shape
    steps = LRU_STEPS
    assert bsz == SUBLANES and seq % steps == 0 and seq // steps >= 2
    assert (steps * bsz) % SUB_ROWS == 0
    row = lambda v: v.reshape(1, d)
    operands = (x, row(g_pre), w_xy, w_gb, conv_w, row(conv_b), w_ri, row(b_r), row(b_i), row(lam),
                w_b)
    hbm = pl.BlockSpec(memory_space=pl.ANY)
    return pl.pallas_call(
        _lru_branch_kernel,
        grid=(seq // steps,),
        in_specs=[hbm] + [_resident(op.shape) for op in operands[1:]],
        out_specs=hbm,
        out_shape=jax.ShapeDtypeStruct(x.shape, _F32),
        scratch_shapes=[
            pltpu.VMEM((2, steps, bsz, d), _F32),
            pltpu.VMEM((2, steps, bsz, d), _F32),
            pltpu.SemaphoreType.DMA((2,)),
            pltpu.SemaphoreType.DMA((2,)),
            pltpu.VMEM((steps * bsz + (CONV_WIDTH - 1) * bsz, d), _F32),
            pltpu.VMEM((steps * bsz, d + LANES), _BF),
            pltpu.VMEM((bsz, d), _F32),
        ],
        compiler_params=pltpu.CompilerParams(
            dimension_semantics=("arbitrary",),
            vmem_limit_bytes=VMEM_LIMIT_BYTES),
        name="lru_branch",
    )(*operands)


def _gmlp_merge(x, mb, g_pre, g_post, w_uv, w_ga, ln_g, ln_b, w_s, b_s, w_a, w_out):
    bsz, seq, d = x.shape
    tile = MIX_TILE
    assert seq % tile == 0 and tile % SUB_ROWS == 0
    row = lambda v: v.reshape(1, d)
    operands = (x, mb, row(g_pre), row(g_post), w_uv, w_ga, row(ln_g), row(ln_b), w_s, b_s, w_a,
                w_out)
    x_spec = pl.BlockSpec((1, tile, d), lambda b, t: (b, t, 0))
    return pl.pallas_call(
        _gmlp_merge_kernel,
        grid=(bsz, seq // tile),
        in_specs=[x_spec, x_spec] + [_resident(op.shape) for op in operands[2:]],
        out_specs=x_spec,
        out_shape=jax.ShapeDtypeStruct(x.shape, x.dtype),
        scratch_shapes=[
            pltpu.VMEM((tile, d + LANES), _BF),
            pltpu.VMEM((tile, d + LANES), _BF),
        ],
        compiler_params=pltpu.CompilerParams(
            dimension_semantics=("arbitrary", "arbitrary"),
            vmem_limit_bytes=VMEM_LIMIT_BYTES),
        name="gmlp_merge",
    )(*operands)


def _ffn_ple(x, p, g_fpre, g_fpost, w_gate, w_up, w_down, g_ppre, g_ppost, w_pin, w_pgate):
    n, d = x.shape
    tile = FFN_TILE
    assert n % tile == 0 and tile % SUB_ROWS == 0
    row = lambda v: v.reshape(1, d)
    operands = (x, p, row(g_fpre), row(g_fpost), w_gate, w_up, w_down, row(g_ppre),
                row(g_ppost), w_pin, w_pgate)
    x_spec = pl.BlockSpec((tile, d), lambda i: (i, 0))
    p_spec = pl.BlockSpec((tile, p.shape[1]), lambda i: (i, 0))
    in_specs = [x_spec, p_spec] + [_resident(op.shape) for op in operands[2:]]
    return pl.pallas_call(
        _ffn_ple_kernel,
        grid=(n // tile,),
        in_specs=in_specs,
        out_specs=x_spec,
        out_shape=jax.ShapeDtypeStruct(x.shape, x.dtype),
        compiler_params=pltpu.CompilerParams(
            dimension_semantics=("arbitrary",),
            vmem_limit_bytes=VMEM_LIMIT_BYTES),
        name="ffn_ple",
    )(*operands)


def kernel(x, p, norm_mix_pre, norm_mix_post, w_in, gmlp_ln_g, gmlp_ln_b, gmlp_w_s, gmlp_b_s, conv_w, conv_b, lru_w_r, lru_b_r, lru_w_i, lru_b_i, lru_lambda, w_branch_a, w_branch_b, w_out, norm_ffn_pre, norm_ffn_post, ffn_w_gate, ffn_w_up, ffn_w_down, norm_ple_pre, norm_ple_post, ple_w_in, ple_w_gate):
    bsz, seq, d = x.shape
    depth = w_in.shape[0]
    bf = lambda w: _odd_pitch(w.astype(_BF))
    for l in range(depth):
        w_ri = jnp.concatenate(
            [_pair_block_diag(lru_w_r[l]), _pair_block_diag(lru_w_i[l])], axis=2).astype(_BF)
        b_s = jnp.repeat(gmlp_b_s[l].T, LANES, axis=1)
        mb = _lru_branch(x, norm_mix_pre[l], bf(w_in[l, :, 2 * d:4 * d]), bf(w_in[l, :, 5 * d:6 * d]),
                         conv_w[l], conv_b[l], w_ri, lru_b_r[l], lru_b_i[l], lru_lambda[l],
                         bf(w_branch_b[l]))
        x = _gmlp_merge(x, mb, norm_mix_pre[l], norm_mix_post[l], bf(w_in[l, :, 0:2 * d]),
                        bf(w_in[l, :, 4 * d:5 * d]), gmlp_ln_g[l], gmlp_ln_b[l], gmlp_w_s[l], b_s,
                        bf(w_branch_a[l]), bf(w_out[l]))
        x = _ffn_ple(x.reshape(bsz * seq, d), p[l].reshape(bsz * seq, -1), norm_ffn_pre[l],
                     norm_ffn_post[l], ffn_w_gate[l].astype(_BF), ffn_w_up[l].astype(_BF),
                     bf(ffn_w_down[l]), norm_ple_pre[l], norm_ple_post[l],
                     bf(ple_w_in[l]), bf(ple_w_gate[l])).reshape(bsz, seq, d)
    return x
```

```python
import jax
import jax.numpy as jnp
import numpy as np
from jax import lax
from jax.experimental import pallas as pl
from jax.experimental.pallas import tpu as pltpu

D_MODEL = 1024
CHUNK = 64
GMLP_BLOCK = 128
GMLP_GROUPS = 8
LRU_HEADS = 16
LRU_HEAD_DIM = D_MODEL // LRU_HEADS
CONV_WIDTH = 4
LRU_C = 8.0
EPS = 1e-6

LANES = 128
SUBLANES = 8
N_LANE_CHUNKS = D_MODEL // LANES
VMEM_LIMIT_BYTES = 56 * 1024 * 1024

SUB_ROWS = 256
MXU_COLS = 256
LRU_STEPS = 128
MIX_TILE = 1024
FFN_TILE = 1024

_BF = jnp.bfloat16
_F32 = jnp.float32


def _dot(a, b):
    return jnp.dot(a, b, preferred_element_type=_F32)


def _rms_norm(x, g):
    y = x * lax.rsqrt(jnp.mean(x * x, axis=-1, keepdims=True) + EPS)
    return y * g


def _layer_norm(x, g, b):
    mu = jnp.mean(x, axis=-1, keepdims=True)
    var = jnp.mean(jnp.square(x - mu), axis=-1, keepdims=True)
    return (x - mu) * lax.rsqrt(var + EPS) * g + b


_GELU_C1 = float(np.sqrt(2.0 / np.pi))
_GELU_C3 = float(np.sqrt(2.0 / np.pi) * 0.044715)


def _gelu(x):
    t = jnp.tanh(x * (_GELU_C1 + _GELU_C3 * (x * x)))
    hx = 0.5 * x
    return hx + hx * t


def _sigmoid(x):
    return 0.5 * jnp.tanh(0.5 * x) + 0.5


def _silu(x):
    hx = 0.5 * x
    return hx + hx * jnp.tanh(hx)


def _sqrt_nonneg(z):
    return jnp.where(z > 0.0, z * lax.rsqrt(z), 0.0)


def _window_copies(hbm, vmem, sem, step, slot, to_vmem):
    steps = vmem.shape[1]
    copies = []
    for b in range(hbm.shape[0]):
        in_hbm = hbm.at[b, pl.ds(step * steps, steps), :]
        in_vmem = vmem.at[slot, :, b, :]
        src, dst = (in_hbm, in_vmem) if to_vmem else (in_vmem, in_hbm)
        copies.append(pltpu.make_async_copy(src, dst, sem.at[slot]))
    return copies


def _start_all(copies):
    for k, c in enumerate(copies):
        c.start(priority=k % 2)


def _emit_skewed(chains, skew):
    depth = max(len(chain) for chain in chains)
    for t in range(depth + (len(chains) - 1) * skew):
        for s, chain in enumerate(chains):
            k = t - s * skew
            if 0 <= k < len(chain):
                chain[k]()


def _lru_branch_kernel(x_hbm, g_pre_ref, w_xy_ref, w_gb_ref, conv_w_ref, conv_b_ref, w_ri_ref,
                       b_r_ref, b_i_ref, lam_ref, w_b_ref, o_hbm,
                       xin, out, sem_in, sem_out, xr_scr, b_scr, hc_scr):
    steps, bsz, d = xin.shape[1:]
    sub_steps = SUB_ROWS // bsz
    tail = (CONV_WIDTH - 1) * bsz
    n_piece = d // MXU_COLS
    chunks_per_piece = MXU_COLS // LANES
    i = pl.program_id(0)
    n = pl.num_programs(0)
    slot = lax.rem(i, 2)

    def loads(step, slot):
        return _window_copies(x_hbm, xin, sem_in, step, slot, to_vmem=True)

    def stores(step, slot):
        return _window_copies(o_hbm, out, sem_out, step, slot, to_vmem=False)

    @pl.when(i == 0)
    def _():
        _start_all(loads(0, 0))
        xr_scr[0:tail, :] = jnp.zeros((tail, d), _F32)
        hc_scr[...] = jnp.zeros_like(hc_scr)

    @pl.when(i + 1 < n)
    def _():
        _start_all(loads(i + 1, 1 - slot))

    for c in loads(i, slot):
        c.wait()

    @pl.when(i >= 2)
    def _():
        for c in stores(i - 2, slot):
            c.wait()

    neg_c_sp = -LRU_C * jax.nn.softplus(-lam_ref[...])
    h_state = [hc_scr[:, c * LANES:(c + 1) * LANES] for c in range(N_LANE_CHUNKS)]

    def sub_tile_chain(s):
        r0 = s * SUB_ROWS
        window = slice(s * sub_steps, (s + 1) * sub_steps)
        v = {}

        def norm():
            x = xin[slot, window].reshape(SUB_ROWS, d)
            v["hb"] = _rms_norm(x, g_pre_ref[...]).astype(_BF)

        def project(j):
            pc = slice(j * MXU_COLS, (j + 1) * MXU_COLS)
            xr_scr[tail + r0:tail + r0 + SUB_ROWS, pc] = _dot(v["hb"], w_xy_ref[:, pc])
            v["yr", j] = _dot(v["hb"], w_xy_ref[:, d + j * MXU_COLS:d + (j + 1) * MXU_COLS])

        def recur(j):
            yg = _gelu(v.pop(("yr", j)))
            for cc in range(chunks_per_piece):
                c = j * chunks_per_piece + cc
                cols = slice(c * LANES, (c + 1) * LANES)
                xc = conv_b_ref[:, cols]
                for k in range(CONV_WIDTH):
                    xc = xc + conv_w_ref[k:k + 1, cols] * xr_scr[r0 + k * bsz:r0 + k * bsz + SUB_ROWS, cols]
                ri = _dot(xc.astype(_BF), w_ri_ref[c])
                r = _sigmoid(ri[:, :LANES] + b_r_ref[:, cols])
                gi = _sigmoid(ri[:, LANES:] + b_i_ref[:, cols])
                log_a = neg_c_sp[:, cols] * r
                a = jnp.exp(log_a)
                gx = _sqrt_nonneg(-jnp.tanh(log_a) * (a * a + 1.0)) * (gi * xc)
                h = h_state[c]
                hs = []
                for t in range(sub_steps):
                    h = a[t * bsz:(t + 1) * bsz] * h + gx[t * bsz:(t + 1) * bsz]
                    hs.append(h)
                h_state[c] = h
                gated = jnp.concatenate(hs, axis=0) * yg[:, cc * LANES:(cc + 1) * LANES]
                b_scr[r0:r0 + SUB_ROWS, cols] = gated.astype(_BF)

        def gate():
            v["gb"] = _sigmoid(_dot(v["hb"], w_gb_ref[:, 0:d]))

        def out_project():
            res = v.pop("gb") * _dot(b_scr[r0:r0 + SUB_ROWS, 0:d], w_b_ref[:, 0:d])
            out[slot, window] = res.reshape(sub_steps, bsz, d)

        def first():
            norm()
            project(0)

        def middle(j):
            def stage():
                project(j + 1)
                if j + 2 == n_piece:
                    gate()
                recur(j)
            return stage

        return ([first] + [middle(j) for j in range(n_piece - 1)]
                + [lambda: recur(n_piece - 1), out_project])

    chains = [sub_tile_chain(s) for s in range(steps // sub_steps)]
    _emit_skewed(chains, skew=len(chains[0]) - 2)
    for c in range(N_LANE_CHUNKS):
        hc_scr[:, c * LANES:(c + 1) * LANES] = h_state[c]
    xr_scr[0:tail, :] = xr_scr[steps * bsz:steps * bsz + tail, :]

    _start_all(stores(i, slot))

    @pl.when(i == n - 1)
    def _():
        for c in stores(i - 1, 1 - slot):
            c.wait()
        for c in stores(i, slot):
            c.wait()


def _gmlp_merge_kernel(x_ref, mb_ref, g_pre_ref, g_post_ref, w_uv_ref, w_ga_ref, ln_g_ref, ln_b_ref,
                       w_s_ref, b_s_ref, w_a_ref, w_out_ref, o_ref, vn_scr, a_scr):
    tile = x_ref.shape[1]
    n_blk = SUB_ROWS // GMLP_BLOCK
    d = D_MODEL
    ci = lax.broadcasted_iota(jnp.int32, (GMLP_BLOCK, GMLP_BLOCK), 0) // CHUNK
    cj = lax.broadcasted_iota(jnp.int32, (GMLP_BLOCK, GMLP_BLOCK), 1) // CHUNK
    w_s = [jnp.where(cj <= ci, w_s_ref[g], 0.0).astype(_BF) for g in range(GMLP_GROUPS)]

    def sub_tile_chain(s):
        rows = slice(s * SUB_ROWS, (s + 1) * SUB_ROWS)
        r0 = s * SUB_ROWS
        v = {}

        def norm_v():
            v["hb"] = _rms_norm(x_ref[0, rows, :], g_pre_ref[...]).astype(_BF)
            v["v"] = _dot(v["hb"], w_uv_ref[:, d:2 * d])

        def ln_u():
            v["u"] = _dot(v["hb"], w_uv_ref[:, 0:d])
            vn_scr[rows, 0:d] = _layer_norm(_gelu(v.pop("v")), ln_g_ref[...], ln_b_ref[...]).astype(_BF)

        def spatial():
            v["ga"] = _dot(v["hb"], w_ga_ref[:, 0:d])
            ug = _gelu(v.pop("u"))
            for g in range(GMLP_GROUPS):
                cols = slice(g * LANES, (g + 1) * LANES)
                rhs = jnp.concatenate(
                    [vn_scr[r0 + n * GMLP_BLOCK:r0 + (n + 1) * GMLP_BLOCK, cols] for n in range(n_blk)],
                    axis=1)
                mixed = _dot(w_s[g], rhs)
                for n in range(n_blk):
                    blk = slice(n * GMLP_BLOCK, (n + 1) * GMLP_BLOCK)
                    m = mixed[:, n * LANES:(n + 1) * LANES] + b_s_ref[:, cols]
                    a_scr[r0 + n * GMLP_BLOCK:r0 + (n + 1) * GMLP_BLOCK, cols] = (ug[blk, cols] * m).astype(_BF)

        def a_project():
            v["m"] = (_sigmoid(v.pop("ga")) * _dot(a_scr[rows, 0:d], w_a_ref[:, 0:d])
                      + mb_ref[0, rows, :])

        def o_project():
            v["y"] = _dot(v.pop("m").astype(_BF), w_out_ref[:, 0:d])

        def residual():
            o_ref[0, rows, :] = x_ref[0, rows, :] + _rms_norm(v.pop("y"), g_post_ref[...])

        return [norm_v, ln_u, spatial, a_project, o_project, residual]

    chains = [sub_tile_chain(s) for s in range(tile // SUB_ROWS)]
    _emit_skewed(chains, skew=len(chains[0]) // 2)


def _ffn_ple_kernel(x_ref, p_ref, g_fpre_ref, g_fpost_ref, w_gate_ref, w_up_ref, w_down_ref,
                    g_ppre_ref, g_ppost_ref, w_pin_ref, w_pgate_ref, o_ref):
    d = D_MODEL
    tile = x_ref.shape[0]

    def sub_tile_chain(s):
        rows = slice(s * SUB_ROWS, (s + 1) * SUB_ROWS)
        v = {}

        def up():
            h = _rms_norm(x_ref[rows, :], g_fpre_ref[...]).astype(_BF)
            v["g"] = _dot(h, w_gate_ref[...])
            v["u"] = _dot(h, w_up_ref[...])
            v["e"] = _dot(p_ref[rows, :].astype(_BF), w_pin_ref[:, 0:d])

        def down():
            act = _silu(v.pop("g")) * v.pop("u")
            v["f"] = _dot(act.astype(_BF), w_down_ref[:, 0:d])

        def ple_gate():
            x = x_ref[rows, :] + _rms_norm(v.pop("f"), g_fpost_ref[...])
            v["x"] = x
            v["pg"] = _dot(_rms_norm(x, g_ppre_ref[...]).astype(_BF), w_pgate_ref[:, 0:d])

        def residual():
            gate = _sigmoid(v.pop("pg"))
            o_ref[rows, :] = v.pop("x") + _rms_norm(gate * v.pop("e"), g_ppost_ref[...])

        return [up, down, ple_gate, residual]

    chains = [sub_tile_chain(s) for s in range(tile // SUB_ROWS)]
    _emit_skewed(chains, skew=1)


def _resident(shape):
    zeros = (0,) * len(shape)
    return pl.BlockSpec(shape, lambda *_: zeros, pipeline_mode=pl.Buffered(1))


def _odd_pitch(w):
    return jnp.pad(w, ((0, 0), (0, LANES)))


def _pair_block_diag(w):
    w = w.reshape(N_LANE_CHUNKS, 2, LRU_HEAD_DIM, LRU_HEAD_DIM)
    z = jnp.zeros_like(w[:, 0])
    top = jnp.concatenate([w[:, 0], z], axis=2)
    bot = jnp.concatenate([z, w[:, 1]], axis=2)
    return jnp.concatenate([top, bot], axis=1)


def _lru_branch(x, g_pre, w_xy, w_gb, conv_w, conv_b, w_ri, b_r, b_i, lam, w_b):
    bsz, seq, d = x.shape
    steps = LRU_STEPS
    assert bsz == SUBLANES and seq % steps == 0 and seq // steps >= 2
    assert (steps * bsz) % SUB_ROWS == 0
    row = lambda v: v.reshape(1, d)
    operands = (x, row(g_pre), w_xy, w_gb, conv_w, row(conv_b), w_ri, row(b_r), row(b_i), row(lam),
                w_b)
    hbm = pl.BlockSpec(memory_space=pl.ANY)
    return pl.pallas_call(
        _lru_branch_kernel,
        grid=(seq // steps,),
        in_specs=[hbm] + [_resident(op.shape) for op in operands[1:]],
        out_specs=hbm,
        out_shape=jax.ShapeDtypeStruct(x.shape, _F32),
        scratch_shapes=[
            pltpu.VMEM((2, steps, bsz, d), _F32),
            pltpu.VMEM((2, steps, bsz, d), _F32),
            pltpu.SemaphoreType.DMA((2,)),
            pltpu.SemaphoreType.DMA((2,)),
            pltpu.VMEM((steps * bsz + (CONV_WIDTH - 1) * bsz, d), _F32),
            pltpu.VMEM((steps * bsz, d + LANES), _BF),
            pltpu.VMEM((bsz, d), _F32),
        ],
        compiler_params=pltpu.CompilerParams(
            dimension_semantics=("arbitrary",),
            vmem_limit_bytes=VMEM_LIMIT_BYTES),
        name="lru_branch",
    )(*operands)


def _gmlp_merge(x, mb, g_pre, g_post, w_uv, w_ga, ln_g, ln_b, w_s, b_s, w_a, w_out):
    bsz, seq, d = x.shape
    tile = MIX_TILE
    assert seq % tile == 0 and tile % SUB_ROWS == 0
    row = lambda v: v.reshape(1, d)
    operands = (x, mb, row(g_pre), row(g_post), w_uv, w_ga, row(ln_g), row(ln_b), w_s, b_s, w_a,
                w_out)
    x_spec = pl.BlockSpec((1, tile, d), lambda b, t: (b, t, 0))
    return pl.pallas_call(
        _gmlp_merge_kernel,
        grid=(bsz, seq // tile),
        in_specs=[x_spec, x_spec] + [_resident(op.shape) for op in operands[2:]],
        out_specs=x_spec,
        out_shape=jax.ShapeDtypeStruct(x.shape, x.dtype),
        scratch_shapes=[
            pltpu.VMEM((tile, d + LANES), _BF),
            pltpu.VMEM((tile, d + LANES), _BF),
        ],
        compiler_params=pltpu.CompilerParams(
            dimension_semantics=("arbitrary", "arbitrary"),
            vmem_limit_bytes=VMEM_LIMIT_BYTES),
        name="gmlp_merge",
    )(*operands)


def _ffn_ple(x, p, g_fpre, g_fpost, w_gate, w_up, w_down, g_ppre, g_ppost, w_pin, w_pgate):
    n, d = x.shape
    tile = FFN_TILE
    assert n % tile == 0 and tile % SUB_ROWS == 0
    row = lambda v: v.reshape(1, d)
    operands = (x, p, row(g_fpre), row(g_fpost), w_gate, w_up, w_down, row(g_ppre),
                row(g_ppost), w_pin, w_pgate)
    x_spec = pl.BlockSpec((tile, d), lambda i: (i, 0))
    p_spec = pl.BlockSpec((tile, p.shape[1]), lambda i: (i, 0))
    in_specs = [x_spec, p_spec] + [_resident(op.shape) for op in operands[2:]]
    return pl.pallas_call(
        _ffn_ple_kernel,
        grid=(n // tile,),
        in_specs=in_specs,
        out_specs=x_spec,
        out_shape=jax.ShapeDtypeStruct(x.shape, x.dtype),
        compiler_params=pltpu.CompilerParams(
            dimension_semantics=("arbitrary",),
            vmem_limit_bytes=VMEM_LIMIT_BYTES),
        name="ffn_ple",
    )(*operands)


def kernel(x, p, norm_mix_pre, norm_mix_post, w_in, gmlp_ln_g, gmlp_ln_b, gmlp_w_s, gmlp_b_s, conv_w, conv_b, lru_w_r, lru_b_r, lru_w_i, lru_b_i, lru_lambda, w_branch_a, w_branch_b, w_out, norm_ffn_pre, norm_ffn_post, ffn_w_gate, ffn_w_up, ffn_w_down, norm_ple_pre, norm_ple_post, ple_w_in, ple_w_gate):
    bsz, seq, d = x.shape
    depth = w_in.shape[0]
    bf = lambda w: _odd_pitch(w.astype(_BF))
    for l in range(depth):
        w_ri = jnp.concatenate(
            [_pair_block_diag(lru_w_r[l]), _pair_block_diag(lru_w_i[l])], axis=2).astype(_BF)
        b_s = jnp.repeat(gmlp_b_s[l].T, LANES, axis=1)
        mb = _lru_branch(x, norm_mix_pre[l], bf(w_in[l, :, 2 * d:4 * d]), bf(w_in[l, :, 5 * d:6 * d]),
                         conv_w[l], conv_b[l], w_ri, lru_b_r[l], lru_b_i[l], lru_lambda[l],
                         bf(w_branch_b[l]))
        x = _gmlp_merge(x, mb, norm_mix_pre[l], norm_mix_post[l], bf(w_in[l, :, 0:2 * d]),
                        bf(w_in[l, :, 4 * d:5 * d]), gmlp_ln_g[l], gmlp_ln_b[l], gmlp_w_s[l], b_s,
                        bf(w_branch_a[l]), bf(w_out[l]))
        x = _ffn_ple(x.reshape(bsz * seq, d), p[l].reshape(bsz * seq, -1), norm_ffn_pre[l],
                     norm_ffn_post[l], ffn_w_gate[l].astype(_BF), ffn_w_up[l].astype(_BF),
                     bf(ffn_w_down[l]), norm_ple_pre[l], norm_ple_post[l],
                     bf(ple_w_in[l]), bf(ple_w_gate[l])).reshape(bsz, seq, d)
    return x
```
